```python
import jax, jax.numpy as jnp
from jax import lax
import numpy as np

D_MODEL = 1024
BATCH = 2
SEQ = 8192
DEPTH = 2

HEAD_DIM = 64
D_MIX = D_MODEL
D_SB = D_MIX // 2
SB_HEADS = D_SB // HEAD_DIM
D_CONV = D_MIX // 4
CONV_GROUPS = D_CONV // HEAD_DIM
CONV_WIDTH = 3
D_POOL = D_MIX - D_SB - D_CONV
POOL_WINDOWS = (2, 4, 8, 16)
POOL_GROUPS = len(POOL_WINDOWS)
POOL_GROUP_DIM = D_POOL // POOL_GROUPS
D_IN = 3 * D_SB + 3 * D_CONV + D_POOL
SPLITS = (D_SB, 2 * D_SB, 3 * D_SB, 3 * D_SB + D_CONV, 3 * D_SB + 2 * D_CONV, 3 * D_SB + 3 * D_CONV)
D_FF = 4 * D_MODEL
Q_BLOCK = 128
DEEPNORM_ALPHA = (2 * DEPTH) ** 0.25
DEEPNORM_BETA = (8 * DEPTH) ** -0.25
LN_EPS = 1e-5
RMS_EPS = 1e-6

kernel_name = "hybrid_sb_attn_shortconv_pool_deepnorm"


def layer_norm(x, g, b):
    xf = x.astype(jnp.float32)
    mu = jnp.mean(xf, axis=-1, keepdims=True)
    xc = xf - mu
    var = jnp.mean(xc * xc, axis=-1, keepdims=True)
    y = xc * lax.rsqrt(var + LN_EPS) * g.astype(jnp.float32) + b.astype(jnp.float32)
    return y.astype(x.dtype)


def head_group_rmsnorm(o, gain):
    B, S, C = o.shape
    of = o.astype(jnp.float32).reshape(B, S, C // HEAD_DIM, HEAD_DIM)
    of = of * lax.rsqrt(jnp.mean(of * of, axis=-1, keepdims=True) + RMS_EPS)
    return (of.reshape(B, S, C) * gain.astype(jnp.float32)).astype(o.dtype)


def stick_breaking_attention(q, k, v):
    B, S, H, Dh = q.shape
    dtype = q.dtype
    scale = Dh ** -0.5
    qf = q.astype(jnp.float32).transpose(0, 2, 1, 3)
    kf = k.astype(jnp.float32).transpose(0, 2, 1, 3)
    vf = v.astype(jnp.float32).transpose(0, 2, 1, 3)
    outs = []
    for start in range(0, S, Q_BLOCK):
        end = start + Q_BLOCK
        qb = qf[:, :, start:end]
        kb = kf[:, :, :end]
        vb = vf[:, :, :end]
        z = jnp.einsum('bhtd,bhsd->bhts', qb, kb) * scale
        t_pos = jnp.arange(start, end)[:, None]
        s_pos = jnp.arange(end)[None, :]
        mask = s_pos < t_pos
        log_om = jnp.where(mask, jax.nn.log_sigmoid(-z), 0.0)
        tail = lax.cumsum(log_om, axis=3, reverse=True) - log_om
        log_a = jax.nn.log_sigmoid(z) + tail
        a = jnp.where(mask, jnp.exp(log_a), 0.0)
        outs.append(jnp.einsum('bhts,bhsd->bhtd', a, vb))
    o = jnp.concatenate(outs, axis=2)
    return o.transpose(0, 2, 1, 3).reshape(B, S, H * Dh).astype(dtype)


def short_conv_mixer(b_gate, c_gate, h, conv_w):
    u = c_gate * h
    S = u.shape[1]
    u_pad = jnp.pad(u, ((0, 0), (CONV_WIDTH - 1, 0), (0, 0)))
    y = u_pad[:, 0:S] * conv_w[0]
    for i in range(1, CONV_WIDTH):
        y = y + u_pad[:, i:i + S] * conv_w[i]
    return b_gate * y


def multiscale_pool_mixer(p, pool_w, pool_scale):
    B, S, C = p.shape
    dtype = p.dtype
    pg = p.astype(jnp.float32).reshape(B, S, POOL_GROUPS, POOL_GROUP_DIM)
    cs = jnp.pad(jnp.cumsum(pg, axis=1), ((0, 0), (1, 0), (0, 0), (0, 0)))
    pos = jnp.arange(S)
    outs = []
    for g, w in enumerate(POOL_WINDOWS):
        lo = jnp.maximum(pos + 1 - w, 0)
        window_sum = cs[:, 1:, g] - cs[:, lo, g]
        count = (pos + 1 - lo).astype(jnp.float32)[None, :, None]
        outs.append(window_sum / count - pg[:, :, g])
    pooled = jnp.stack(outs, axis=2)
    y = jnp.einsum('bsgc,gcd->bsgd', pooled, pool_w.astype(jnp.float32))
    y = y.reshape(B, S, C) * pool_scale.astype(jnp.float32)
    return y.astype(dtype)


def setup_inputs(seed: int = 0) -> dict:
    key = jax.random.key(seed)
    ks = jax.random.split(key, 16)
    f32 = jnp.float32
    x = jax.random.normal(ks[0], (BATCH, SEQ, D_MODEL), f32)
    w_in = jax.random.normal(ks[1], (DEPTH, D_MODEL, D_IN), f32) * D_MODEL ** -0.5
    conv_w = jax.random.normal(ks[2], (DEPTH, CONV_WIDTH, D_CONV), f32) * CONV_WIDTH ** -0.5
    pool_w = jax.random.normal(ks[3], (DEPTH, POOL_GROUPS, POOL_GROUP_DIM, POOL_GROUP_DIM), f32) * POOL_GROUP_DIM ** -0.5
    pool_scale = 1.0 + 0.1 * jax.random.normal(ks[4], (DEPTH, D_POOL), f32)
    mix_norm_g = 1.0 + 0.02 * jax.random.normal(ks[5], (DEPTH, D_MIX), f32)
    w_o = jax.random.normal(ks[6], (DEPTH, D_MIX, D_MODEL), f32) * (D_MIX ** -0.5) * DEEPNORM_BETA
    ln1_g = 1.0 + 0.02 * jax.random.normal(ks[7], (DEPTH, D_MODEL), f32)
    ln1_b = 0.02 * jax.random.normal(ks[8], (DEPTH, D_MODEL), f32)
    w_up = jax.random.normal(ks[9], (DEPTH, D_MODEL, D_FF), f32) * D_MODEL ** -0.5
    w_down = jax.random.normal(ks[10], (DEPTH, D_FF, D_MODEL), f32) * (D_FF ** -0.5) * DEEPNORM_BETA
    ln2_g = 1.0 + 0.02 * jax.random.normal(ks[11], (DEPTH, D_MODEL), f32)
    ln2_b = 0.02 * jax.random.normal(ks[12], (DEPTH, D_MODEL), f32)
    return {"x": x, "w_in": w_in, "conv_w": conv_w, "pool_w": pool_w,
            "pool_scale": pool_scale, "mix_norm_g": mix_norm_g, "w_o": w_o,
            "ln1_g": ln1_g, "ln1_b": ln1_b, "w_up": w_up, "w_down": w_down,
            "ln2_g": ln2_g, "ln2_b": ln2_b}


def reference(x, w_in, conv_w, pool_w, pool_scale, mix_norm_g, w_o,
              ln1_g, ln1_b, w_up, w_down, ln2_g, ln2_b):
    B, S, _ = x.shape
    for l in range(DEPTH):
        proj = jnp.einsum('bsd,de->bse', x, w_in[l])
        q, k, v, b_gate, c_gate, h, p = jnp.split(proj, SPLITS, axis=-1)
        q = q.reshape(B, S, SB_HEADS, HEAD_DIM)
        k = k.reshape(B, S, SB_HEADS, HEAD_DIM)
        v = v.reshape(B, S, SB_HEADS, HEAD_DIM)
        attn_out = stick_breaking_attention(q, k, v)
        conv_out = short_conv_mixer(b_gate, c_gate, h, conv_w[l])
        pool_out = multiscale_pool_mixer(p, pool_w[l], pool_scale[l])
        mix = jnp.concatenate([attn_out, conv_out, pool_out], axis=-1)
        mix = head_group_rmsnorm(mix, mix_norm_g[l])
        mix = jnp.einsum('bse,ed->bsd', mix, w_o[l])
        x = layer_norm(DEEPNORM_ALPHA * x + mix, ln1_g[l], ln1_b[l])
        hid = jnp.square(jax.nn.relu(jnp.einsum('bsd,df->bsf', x, w_up[l])))
        ff = jnp.einsum('bsf,fd->bsd', hid, w_down[l])
        x = layer_norm(DEEPNORM_ALPHA * x + ff, ln2_g[l], ln2_b[l])
    return x
```

```python
import functools
import math

import jax
import jax.numpy as jnp
from jax import lax
from jax.experimental import pallas as pl
from jax.experimental.pallas import tpu as pltpu

F32 = jnp.float32
BF16 = jnp.bfloat16

DEPTH = 2
HEAD_DIM = 64
CONV_WIDTH = 3
POOL_WINDOWS = (2, 4, 8, 16)
DEEPNORM_ALPHA = (2 * DEPTH) ** 0.25
LN_EPS = 1e-5
RMS_EPS = 1e-6
LOG2E = math.log2(math.e)

LANES = 128
MXU_TILE = 256
HALO = 16

PROJ_ROWS = 512
ATTN_BLOCK = MXU_TILE
MIX_ROWS = 256
FFN_ROWS = 256
FFN_CHUNK = 512
VMEM_LIMIT = 56 * 1024 * 1024


def _split_bf16(x):
    hi = x.astype(BF16)
    lo = (x - hi.astype(F32)).astype(BF16)
    return hi, lo


def _layer_norm(y, g, b):
    mu = jnp.mean(y, axis=-1, keepdims=True)
    yc = y - mu
    var = jnp.mean(yc * yc, axis=-1, keepdims=True)
    return yc * lax.rsqrt(var + LN_EPS) * g + b


def _proj_kernel(x_ref, w_ref, qkv_ref, cp_ref, *, d_sb):
    xb = x_ref[...].astype(BF16)
    n_qkv = qkv_ref.shape[1]
    n_cp = cp_ref.shape[1]
    q_scale = -LOG2E * HEAD_DIM ** -0.5
    for c in range(0, n_qkv, d_sb):
        p = jnp.dot(xb, w_ref[:, c:c + d_sb], preferred_element_type=F32)
        if c == 0:
            p = p * q_scale
        qkv_ref[:, c:c + d_sb] = p.astype(BF16)
    for c in range(0, n_cp, d_sb):
        cp_ref[:, c:c + d_sb] = jnp.dot(xb, w_ref[:, n_qkv + c:n_qkv + c + d_sb],
                                        preferred_element_type=F32)


def _proj(xf, w_bf16, d_sb):
    m, d = xf.shape
    n_qkv = 3 * d_sb
    n_cp = w_bf16.shape[1] - n_qkv
    return pl.pallas_call(
        functools.partial(_proj_kernel, d_sb=d_sb),
        grid=(m // PROJ_ROWS,),
        in_specs=[pl.BlockSpec((PROJ_ROWS, d), lambda i: (i, 0)),
                  pl.BlockSpec(w_bf16.shape, lambda i: (0, 0))],
        out_specs=[pl.BlockSpec((PROJ_ROWS, n_qkv), lambda i: (i, 0)),
                   pl.BlockSpec((PROJ_ROWS, n_cp), lambda i: (i, 0))],
        out_shape=[jax.ShapeDtypeStruct((m, n_qkv), BF16),
                   jax.ShapeDtypeStruct((m, n_cp), F32)],
        compiler_params=pltpu.CompilerParams(dimension_semantics=("arbitrary",),
                                             vmem_limit_bytes=VMEM_LIMIT),
        name="proj",
    )(xf, w_bf16)


def _sb_block(qh, kj, vj, u, r, mask):
    nz = lax.dot_general(qh, kj, (((1,), (1,)), ((), ())), preferred_element_type=F32)
    e = jnp.exp2(-jnp.abs(nz))
    lom = jnp.minimum(nz, 0.0) - jnp.log(1.0 + e) * LOG2E
    if mask is not None:
        lom = jnp.where(mask, lom, 0.0)
    hi, lo = _split_bf16(lom)
    c = (jnp.dot(hi, u, preferred_element_type=F32)
         + jnp.dot(lo, u, preferred_element_type=F32))
    a = jnp.exp2(c + r - nz)
    if mask is not None:
        a = jnp.where(mask, a, 0.0)
    pv = jnp.dot(a.astype(BF16), vj, preferred_element_type=F32)
    return pv, r + c[:, 0:1]


def _attn_kernel(q_ref, k_ref, v_ref, u_ref, o_ref, acc_ref, r_ref):
    i = pl.program_id(2)
    t = q_ref.shape[1]
    lane = lax.broadcasted_iota(jnp.int32, (t, LANES), 1)
    first = lane < HEAD_DIM
    q2 = q_ref[0].astype(F32)
    qh = (jnp.where(first, q2, 0.0).astype(BF16), jnp.where(first, 0.0, q2).astype(BF16))
    u = u_ref[...]
    causal = (lax.broadcasted_iota(jnp.int32, (t, t), 1)
              < lax.broadcasted_iota(jnp.int32, (t, t), 0))

    start = pl.multiple_of(i * t, t)
    kj = k_ref[0, pl.ds(start, t), :]
    vj = v_ref[0, pl.ds(start, t), :]
    for h in range(2):
        pv, r = _sb_block(qh[h], kj, vj, u, jnp.zeros((t, 1), F32), causal)
        acc_ref[h] = pv
        r_ref[h] = r

    def body(jj, carry):
        s = pl.multiple_of((i - 1 - jj) * t, t)
        kb = k_ref[0, pl.ds(s, t), :]
        vb = v_ref[0, pl.ds(s, t), :]
        for h in range(2):
            pv, r = _sb_block(qh[h], kb, vb, u, r_ref[h], None)
            acc_ref[h] += pv
            r_ref[h] = r
        return carry

    lax.fori_loop(0, i, body, 0)
    o_ref[0] = jnp.where(first, acc_ref[0], acc_ref[1])


def _attn(qkv, d_sb):
    b, s, _ = qkv.shape
    t = ATTN_BLOCK
    pairs = d_sb // LANES
    u = jnp.tril(jnp.ones((t, t), F32)).astype(BF16)
    return pl.pallas_call(
        _attn_kernel,
        grid=(b, pairs, s // t),
        in_specs=[pl.BlockSpec((1, t, LANES), lambda bb, p, i: (bb, i, p)),
                  pl.BlockSpec((1, s, LANES), lambda bb, p, i: (bb, 0, pairs + p)),
                  pl.BlockSpec((1, s, LANES), lambda bb, p, i: (bb, 0, 2 * pairs + p)),
                  pl.BlockSpec((t, t), lambda bb, p, i: (0, 0))],
        out_specs=pl.BlockSpec((1, t, LANES), lambda bb, p, i: (bb, i, p)),
        out_shape=jax.ShapeDtypeStruct((b, s, d_sb), F32),
        scratch_shapes=[pltpu.VMEM((2, t, LANES), F32), pltpu.VMEM((2, t, 1), F32)],
        compiler_params=pltpu.CompilerParams(
            dimension_semantics=("arbitrary", "arbitrary", "arbitrary"),
            vmem_limit_bytes=VMEM_LIMIT),
        name="sb_attn",
    )(qkv, qkv, qkv, u)


def _mix_kernel(x_ref, attn_ref, cp_ref, halo_ref, convw_ref, poolw_ref, pscale_ref, gain_ref,
                ones_ref, wo_ref, g_ref, b_ref, o_ref, *, seq_len):
    i = pl.program_id(0)
    tm = x_ref.shape[0]
    dc = convw_ref.shape[1]
    tiles_per_seq = seq_len // tm
    at_seq_start = (i % tiles_per_seq) == 0
    keep = jnp.where(at_seq_start, 0.0, 1.0)

    cp = cp_ref[...]
    halo = halo_ref[...] * keep
    b_gate = cp[:, 0:dc]
    u_ext = jnp.concatenate([halo[:, dc:2 * dc] * halo[:, 2 * dc:3 * dc],
                             cp[:, dc:2 * dc] * cp[:, 2 * dc:3 * dc]], axis=0)
    p_ext = jnp.concatenate([halo[:, 3 * dc:4 * dc], cp[:, 3 * dc:4 * dc]], axis=0)

    cw = convw_ref[...]
    y = u_ext * cw[CONV_WIDTH - 1:CONV_WIDTH, :]
    for d in range(1, CONV_WIDTH):
        y = y + pltpu.roll(u_ext, d, 0) * cw[CONV_WIDTH - 1 - d:CONV_WIDTH - d, :]
    conv_out = b_gate * y[HALO:, :]

    lane = lax.broadcasted_iota(jnp.int32, (tm + HALO, dc), 1)
    gdim = dc // len(POOL_WINDOWS)
    wsum = None
    level = p_ext
    width = 1
    for g, w in enumerate(POOL_WINDOWS):
        while width < w:
            level = level + pltpu.roll(level, width, 0)
            width *= 2
        wsum = level if wsum is None else jnp.where(lane < g * gdim, wsum, level)
    pos = (i % tiles_per_seq) * tm + lax.broadcasted_iota(jnp.int32, (tm, dc), 0)
    lane_t = lax.broadcasted_iota(jnp.int32, (tm, dc), 1)
    win = jnp.full((tm, dc), POOL_WINDOWS[0], jnp.int32)
    for g, w in enumerate(POOL_WINDOWS[1:], start=1):
        win = jnp.where(lane_t < g * gdim, win, w)
    count = jnp.minimum(pos + 1, win).astype(F32)
    pooled = wsum[HALO:, :] / count - cp[:, 3 * dc:4 * dc]
    pool_out = jnp.dot(pooled.astype(BF16), poolw_ref[...], preferred_element_type=F32) * pscale_ref[...]

    mix = jnp.concatenate([attn_ref[...], conv_out, pool_out], axis=-1)

    sq_hi, sq_lo = _split_bf16(mix * mix)
    ones_blk = ones_ref[...]
    ssq = []
    for c in range(0, mix.shape[1], MXU_TILE):
        ssq.append(jnp.dot(sq_hi[:, c:c + MXU_TILE], ones_blk, preferred_element_type=F32)
                   + jnp.dot(sq_lo[:, c:c + MXU_TILE], ones_blk, preferred_element_type=F32))
    ssq = jnp.concatenate(ssq, axis=-1)
    mixn = mix * lax.rsqrt(ssq * (1.0 / HEAD_DIM) + RMS_EPS) * gain_ref[...]

    y = DEEPNORM_ALPHA * x_ref[...] + jnp.dot(mixn.astype(BF16), wo_ref[...], preferred_element_type=F32)
    o_ref[...] = _layer_norm(y, g_ref[...], b_ref[...])


def _mix(xf, attn, cp, conv_w, pool_w, pool_scale, gain, w_o_bf16, ln_g, ln_b, seq_len):
    m, d = xf.shape
    dc = conv_w.shape[1]
    tm = MIX_ROWS
    halo_blocks = tm // HALO
    pool_bd = jax.scipy.linalg.block_diag(*[pool_w[g] for g in range(pool_w.shape[0])]).astype(BF16)
    grp = jnp.arange(MXU_TILE) // HEAD_DIM
    ones_blk = (grp[:, None] == grp[None, :]).astype(BF16)
    row = lambda v: v.reshape(1, -1)
    const = lambda shape: pl.BlockSpec(shape, lambda i: (0, 0))
    return pl.pallas_call(
        functools.partial(_mix_kernel, seq_len=seq_len),
        grid=(m // tm,),
        in_specs=[pl.BlockSpec((tm, d), lambda i: (i, 0)),
                  pl.BlockSpec((tm, attn.shape[1]), lambda i: (i, 0)),
                  pl.BlockSpec((tm, cp.shape[1]), lambda i: (i, 0)),
                  pl.BlockSpec((HALO, cp.shape[1]), lambda i: (jnp.maximum(i * halo_blocks - 1, 0), 0)),
                  const(conv_w.shape), const(pool_bd.shape), const((1, dc)), const((1, d)),
                  const(ones_blk.shape), const(w_o_bf16.shape), const((1, d)), const((1, d))],
        out_specs=pl.BlockSpec((tm, d), lambda i: (i, 0)),
        out_shape=jax.ShapeDtypeStruct((m, d), F32),
        compiler_params=pltpu.CompilerParams(dimension_semantics=("arbitrary",),
                                             vmem_limit_bytes=VMEM_LIMIT),
        name="mix",
    )(xf, attn, cp, cp, conv_w, pool_bd, row(pool_scale), row(gain), ones_blk, w_o_bf16,
      row(ln_g), row(ln_b))


def _ffn_kernel(x_ref, wu_ref, wd_ref, g_ref, b_ref, o_ref, acc_ref):
    x = x_ref[...]
    xb = x.astype(BF16)
    acc_ref[...] = DEEPNORM_ALPHA * x
    for off in range(0, wu_ref.shape[1], FFN_CHUNK):
        h = jnp.dot(xb, wu_ref[:, off:off + FFN_CHUNK], preferred_element_type=F32)
        h = jnp.square(jnp.maximum(h, 0.0)).astype(BF16)
        acc_ref[...] += jnp.dot(h, wd_ref[off:off + FFN_CHUNK, :], preferred_element_type=F32)
    o_ref[...] = _layer_norm(acc_ref[...], g_ref[...], b_ref[...])


def _ffn(xf, w_up_bf16, w_down_bf16, ln_g, ln_b):
    m, d = xf.shape
    tm = FFN_ROWS
    row = lambda v: v.reshape(1, -1)
    resident = lambda shape: pl.BlockSpec(shape, lambda i: (0, 0), pipeline_mode=pl.Buffered(1))
    return pl.pallas_call(
        _ffn_kernel,
        grid=(m // tm,),
        in_specs=[pl.BlockSpec((tm, d), lambda i: (i, 0)),
                  resident(w_up_bf16.shape), resident(w_down_bf16.shape),
                  pl.BlockSpec((1, d), lambda i: (0, 0)), pl.BlockSpec((1, d), lambda i: (0, 0))],
        out_specs=pl.BlockSpec((tm, d), lambda i: (i, 0)),
        out_shape=jax.ShapeDtypeStruct((m, d), F32),
        scratch_shapes=[pltpu.VMEM((tm, d), F32)],
        compiler_params=pltpu.CompilerParams(dimension_semantics=("arbitrary",),
                                             vmem_limit_bytes=VMEM_LIMIT),
        name="ffn",
    )(xf, w_up_bf16, w_down_bf16, row(ln_g), row(ln_b))


def kernel(x, w_in, conv_w, pool_w, pool_scale, mix_norm_g, w_o, ln1_g, ln1_b, w_up, w_down, ln2_g, ln2_b):
    b, s, d = x.shape
    depth = w_in.shape[0]
    dc = conv_w.shape[2]
    d_sb = (w_in.shape[2] - 4 * dc) // 3
    assert d_sb % LANES == 0 and s % ATTN_BLOCK == 0 and s % MIX_ROWS == 0
    assert (b * s) % PROJ_ROWS == 0 and (b * s) % FFN_ROWS == 0 and MIX_ROWS % HALO == 0
    xf = x.reshape(b * s, d)
    for l in range(depth):
        qkv, cp = _proj(xf, w_in[l].astype(BF16), d_sb)
        attn = _attn(qkv.reshape(b, s, 3 * d_sb), d_sb).reshape(b * s, d_sb)
        xf = _mix(xf, attn, cp, conv_w[l], pool_w[l], pool_scale[l], mix_norm_g[l],
                  w_o[l].astype(BF16), ln1_g[l], ln1_b[l], s)
        xf = _ffn(xf, w_up[l].astype(BF16), w_down[l].astype(BF16), ln2_g[l], ln2_b[l])
    return xf.reshape(b, s, d)
```
